```python
import math
import jax, jax.numpy as jnp
from jax import lax
import numpy as np

D_MODEL = 1024
BATCH = 8
SEQ = 8192
DEPTH = 1

MEM_LEN = 256
LN_EPS = 1e-5
DN_ALPHA = (2.0 * DEPTH) ** 0.25
DN_BETA = (8.0 * DEPTH) ** -0.25

DIL_GROUPS = ((128, 1), (512, 4), (2048, 16))
N_DIL_GROUPS = 3
DIL_HEADS = 4
DIL_HEAD_DIM = 128
DIL_WIDTH = DIL_HEADS * DIL_HEAD_DIM
DIL_GROUP_COLS = N_DIL_GROUPS * DIL_HEADS * DIL_HEAD_DIM
DIL_QKV_COLS = 3 * DIL_GROUP_COLS
DIL_BLOCK = 128

REL_BUCKETS = 32
REL_MAX_DIST = 2048

HG_HEADS = 4
HG_DK = 128
HG_DV = 128
HG_WIDTH = HG_HEADS * HG_DV
HG_COLS = 4 * HG_WIDTH
HG_I_OFF = DIL_QKV_COLS + HG_WIDTH
HG_CHUNK = 64

MEM_HEADS = 4
MEM_HEAD_DIM = 128
MEM_WIDTH = MEM_HEADS * MEM_HEAD_DIM

N_BRANCHES = 3
GATE_COLS = N_BRANCHES * D_MODEL
IN_COLS = DIL_QKV_COLS + HG_COLS + MEM_WIDTH + GATE_COLS
IN_SPLITS = (DIL_QKV_COLS, DIL_QKV_COLS + HG_COLS, DIL_QKV_COLS + HG_COLS + MEM_WIDTH)

PEER_HEADS = 8
PEER_N_KEYS = 128
PEER_N_EXPERTS = PEER_N_KEYS * PEER_N_KEYS
PEER_DHALF = 128
PEER_DKEY = 2 * PEER_DHALF
PEER_TOPK = 16
PEER_TOK_BLOCK = 128

kernel_name = 'hybrid_dilated_hgrn2_mem_peer_deepnorm'


def _layer_norm(t, g, b):
    t32 = t.astype(jnp.float32)
    mu = jnp.mean(t32, axis=-1, keepdims=True)
    var = jnp.mean(jnp.square(t32 - mu), axis=-1, keepdims=True)
    return ((t32 - mu) * lax.rsqrt(var + LN_EPS) * g.astype(jnp.float32) + b.astype(jnp.float32)).astype(t.dtype)


def _t5_bucket(dist):
    max_exact = REL_BUCKETS // 2
    n = np.maximum(dist, 0)
    large = max_exact + (np.log(np.maximum(n, 1) / max_exact) / math.log(REL_MAX_DIST / max_exact)
                         * (REL_BUCKETS - max_exact)).astype(np.int32)
    large = np.minimum(large, REL_BUCKETS - 1)
    return np.where(n < max_exact, n, large).astype(np.int32)


def _dilated_group(q, k, v, bias_table, window, dilation):
    b, s, h, hd = q.shape
    span = dilation * DIL_BLOCK
    sp = -(-s // span) * span
    sub = sp // dilation
    nb = sub // DIL_BLOCK

    def to_blocks(t):
        t = jnp.pad(t, ((0, 0), (0, sp - s), (0, 0), (0, 0)))
        t = t.reshape(b, sub, dilation, h, hd).transpose(0, 2, 1, 3, 4)
        return t.reshape(b, dilation, nb, DIL_BLOCK, h, hd)

    def with_prev(t):
        prev = jnp.pad(t[:, :, :-1], ((0, 0), (0, 0), (1, 0), (0, 0), (0, 0), (0, 0)))
        return jnp.concatenate([prev, t], axis=3)

    qb = to_blocks(q)
    kb = with_prev(to_blocks(k))
    vb = with_prev(to_blocks(v))
    qi = np.arange(DIL_BLOCK)[:, None]
    kj = np.arange(2 * DIL_BLOCK)[None, :]
    delta = qi + DIL_BLOCK - kj
    in_window = (delta >= 0) & (delta <= window // dilation)
    rel = jnp.transpose(bias_table[_t5_bucket(delta * dilation)], (2, 0, 1)).astype(jnp.float32)
    not_first = np.arange(nb)[:, None, None] > 0
    valid = in_window[None] & (not_first | (kj >= DIL_BLOCK)[None])
    logits = jnp.einsum('brnqhd,brnkhd->brnhqk', qb, kb).astype(jnp.float32) * (hd ** -0.5) + rel[None, None, None]
    logits = jnp.where(valid[None, None, :, None], logits, -jnp.inf)
    lse = jax.nn.logsumexp(logits, axis=-1)
    p = jnp.exp(logits - lse[..., None])
    o = jnp.einsum('brnhqk,brnkhd->brnqhd', p.astype(v.dtype), vb)
    o = o.reshape(b, dilation, sub, h, hd).transpose(0, 2, 1, 3, 4).reshape(b, sp, h, hd)[:, :s]
    lse = lse.transpose(0, 1, 2, 4, 3).reshape(b, dilation, sub, h).transpose(0, 2, 1, 3).reshape(b, sp, h)[:, :s]
    return o, lse


def _dilated_attention(dil, rel_bias):
    b, s, _ = dil.shape
    dil = dil.reshape(b, s, 3, N_DIL_GROUPS, DIL_HEADS, DIL_HEAD_DIM)
    outs, lses = [], []
    for g_idx, (window, dilation) in enumerate(DIL_GROUPS):
        o_g, lse_g = _dilated_group(dil[:, :, 0, g_idx], dil[:, :, 1, g_idx], dil[:, :, 2, g_idx],
                                    rel_bias[:, g_idx * DIL_HEADS:(g_idx + 1) * DIL_HEADS], window, dilation)
        outs.append(o_g)
        lses.append(lse_g)
    w_groups = jax.nn.softmax(jnp.stack(lses, axis=0), axis=0)
    o = jnp.einsum('gbsh,gbshd->bshd', w_groups, jnp.stack(outs, axis=0).astype(jnp.float32))
    return o.reshape(b, s, DIL_WIDTH).astype(dil.dtype)


def _hgrn2(f_raw, i_in, q_in, g_in, lb, norm_w):
    b, s, _ = f_raw.shape
    n = s // HG_CHUNK

    def heads(t):
        return t.astype(jnp.float32).reshape(b, n, HG_CHUNK, HG_HEADS, -1).transpose(0, 3, 1, 2, 4)

    lbh = lb.astype(jnp.float32).reshape(HG_HEADS, 1, 1, HG_DK)
    f = lbh + (1.0 - lbh) * jax.nn.sigmoid(heads(f_raw))
    log_f = jnp.log(f)
    key = 1.0 - f
    qh = heads(q_in)
    ih = heads(i_in)
    a = jnp.cumsum(log_f, axis=3)
    a_mid = a[:, :, :, HG_CHUNK // 2 - 1:HG_CHUNK // 2]
    a_last = a[:, :, :, -1:]
    causal = np.tril(np.ones((HG_CHUNK, HG_CHUNK), dtype=bool))
    sc = jnp.einsum('bhncd,bhnsd->bhncs', qh * jnp.exp(a - a_mid), key * jnp.exp(a_mid - a))
    o_intra = jnp.einsum('bhncs,bhnsv->bhncv', jnp.where(causal, sc, 0.0), ih)
    upd = jnp.einsum('bhncd,bhncv->bhndv', key * jnp.exp(a_last - a), ih)
    decay = jnp.exp(a_last[:, :, :, 0])

    def step(state, inp):
        dec, u = inp
        return dec[..., None] * state + u, state

    init = jnp.zeros((b, HG_HEADS, HG_DK, HG_DV), jnp.float32)
    _, before = lax.scan(step, init, (jnp.moveaxis(decay, 2, 0), jnp.moveaxis(upd, 2, 0)))
    before = jnp.moveaxis(before, 0, 2)
    o_inter = jnp.einsum('bhncd,bhndv->bhncv', qh * jnp.exp(a), before)
    o = o_intra + o_inter
    o = o * lax.rsqrt(jnp.mean(o * o, axis=-1, keepdims=True) + LN_EPS)
    o = o.transpose(0, 2, 3, 1, 4).reshape(b, s, HG_WIDTH)
    o = o * norm_w.astype(jnp.float32) * jax.nn.silu(g_in.astype(jnp.float32))
    return o.astype(f_raw.dtype)


def _memory_attention(q_in, mem, w_mem_kv):
    b, s, _ = q_in.shape
    m = mem.shape[1]
    k, v = jnp.split(jnp.einsum('bmd,de->bme', mem, w_mem_kv), 2, axis=-1)
    q = q_in.reshape(b, s, MEM_HEADS, MEM_HEAD_DIM)
    k = k.reshape(b, m, MEM_HEADS, MEM_HEAD_DIM)
    v = v.reshape(b, m, MEM_HEADS, MEM_HEAD_DIM)
    logits = jnp.einsum('bshd,bmhd->bhsm', q, k).astype(jnp.float32) * (MEM_HEAD_DIM ** -0.5)
    p = jax.nn.softmax(logits, axis=-1)
    return jnp.einsum('bhsm,bmhd->bshd', p.astype(v.dtype), v).reshape(b, s, MEM_WIDTH)


def _peer(h, w_q, sub_keys, u_tab, v_tab):
    b, s, d = h.shape
    q = jnp.einsum('bsd,dk->bsk', h, w_q).reshape(b, s, PEER_HEADS, 2, PEER_DHALF)
    sc = jnp.einsum('bshpk,hpnk->bshpn', q, sub_keys).astype(jnp.float32)
    s1, i1 = lax.top_k(sc[..., 0, :], PEER_TOPK)
    s2, i2 = lax.top_k(sc[..., 1, :], PEER_TOPK)
    cand = (s1[..., :, None] + s2[..., None, :]).reshape(b, s, PEER_HEADS, PEER_TOPK * PEER_TOPK)
    cidx = (i1[..., :, None] * PEER_N_KEYS + i2[..., None, :]).reshape(b, s, PEER_HEADS, PEER_TOPK * PEER_TOPK)
    top, pos = lax.top_k(cand, PEER_TOPK)
    idx = jnp.take_along_axis(cidx, pos, axis=-1)
    gate = jax.nn.softmax(top, axis=-1)
    n_blk = (b * s) // PEER_TOK_BLOCK
    xb = h.reshape(n_blk, PEER_TOK_BLOCK, d)
    ib = idx.reshape(n_blk, PEER_TOK_BLOCK, PEER_HEADS * PEER_TOPK)
    gb = gate.reshape(n_blk, PEER_TOK_BLOCK, PEER_HEADS * PEER_TOPK).astype(h.dtype)

    def block(args):
        xt, it, gt = args
        act = jax.nn.gelu(jnp.einsum('td,ted->te', xt, u_tab[it]), approximate=False)
        return jnp.einsum('te,ted->td', gt * act, v_tab[it])

    return lax.map(block, (xb, ib, gb)).reshape(b, s, d)


def setup_inputs(seed: int = 0) -> dict:
    key = jax.random.key(seed)
    ks = jax.random.split(key, 18)
    f32 = jnp.float32

    def nrm(k, shape, scale):
        return jax.random.normal(k, shape, f32) * scale

    col_scale = np.ones((IN_COLS,), np.float32)
    col_scale[2 * DIL_GROUP_COLS:3 * DIL_GROUP_COLS] = DN_BETA
    col_scale[HG_I_OFF:HG_I_OFF + HG_WIDTH] = DN_BETA
    mem_scale = np.concatenate([np.ones((MEM_WIDTH,)), np.full((MEM_WIDTH,), DN_BETA)]).astype(np.float32)
    d_inv = D_MODEL ** -0.5
    return {
        'x': nrm(ks[0], (BATCH, SEQ, D_MODEL), 1.0),
        'mem': nrm(ks[1], (BATCH, MEM_LEN, D_MODEL), 1.0),
        'w_in': nrm(ks[2], (DEPTH, D_MODEL, IN_COLS), d_inv) * jnp.asarray(col_scale),
        'b_gate': nrm(ks[3], (DEPTH, GATE_COLS), 0.1),
        'w_mem_kv': nrm(ks[4], (DEPTH, D_MODEL, 2 * MEM_WIDTH), d_inv) * jnp.asarray(mem_scale),
        'rel_bias': nrm(ks[5], (REL_BUCKETS, N_DIL_GROUPS * DIL_HEADS), 0.3),
        'hgrn_lb': nrm(ks[6], (DEPTH + 1, HG_HEADS * HG_DK), 0.1),
        'hgrn_norm_w': 1.0 + nrm(ks[7], (DEPTH, HG_WIDTH), 0.05),
        'w_branch': nrm(ks[8], (DEPTH, N_BRANCHES, DIL_WIDTH, D_MODEL), DIL_WIDTH ** -0.5 * DN_BETA),
        'w_out': nrm(ks[9], (DEPTH, D_MODEL, D_MODEL), d_inv * DN_BETA),
        'ln1_g': 1.0 + nrm(ks[10], (DEPTH, D_MODEL), 0.05),
        'ln1_b': nrm(ks[11], (DEPTH, D_MODEL), 0.02),
        'w_peer_q': nrm(ks[12], (DEPTH, D_MODEL, PEER_HEADS * PEER_DKEY), d_inv),
        'peer_sub_keys': nrm(ks[13], (DEPTH, PEER_HEADS, 2, PEER_N_KEYS, PEER_DHALF), PEER_DHALF ** -0.5),
        'peer_u': nrm(ks[14], (DEPTH, PEER_N_EXPERTS, D_MODEL), d_inv),
        'peer_v': nrm(ks[15], (DEPTH, PEER_N_EXPERTS, D_MODEL), DN_BETA),
        'ln2_g': 1.0 + nrm(ks[16], (DEPTH, D_MODEL), 0.05),
        'ln2_b': nrm(ks[17], (DEPTH, D_MODEL), 0.02),
    }


def reference(x, mem, w_in, b_gate, w_mem_kv, rel_bias, hgrn_lb, hgrn_norm_w, w_branch, w_out,
              ln1_g, ln1_b, w_peer_q, peer_sub_keys, peer_u, peer_v, ln2_g, ln2_b):
    lb_layers = jnp.cumsum(jax.nn.softmax(hgrn_lb.astype(jnp.float32), axis=0), axis=0)
    h = x
    for layer in range(DEPTH):
        b, s, _ = h.shape
        proj = jnp.einsum('bsd,de->bse', h, w_in[layer])
        dil, hg, mem_q, gate_logits = jnp.split(proj, IN_SPLITS, axis=-1)
        a_out = _dilated_attention(dil, rel_bias)
        f_raw, i_in, q_in, g_in = jnp.split(hg, 4, axis=-1)
        b_out = _hgrn2(f_raw, i_in, q_in, g_in, lb_layers[layer], hgrn_norm_w[layer])
        c_out = _memory_attention(mem_q, mem, w_mem_kv[layer])
        branches = jnp.stack([a_out, b_out, c_out], axis=2)
        gates = jax.nn.sigmoid((gate_logits + b_gate[layer]).reshape(b, s, N_BRANCHES, D_MODEL))
        widened = jnp.einsum('bsgw,gwd->bsgd', branches, w_branch[layer])
        merged = jnp.sum(gates * widened, axis=2)
        mix = jnp.einsum('bsd,de->bse', merged, w_out[layer])
        h = _layer_norm(DN_ALPHA * h + mix, ln1_g[layer], ln1_b[layer])
        ffn = _peer(h, w_peer_q[layer], peer_sub_keys[layer], peer_u[layer], peer_v[layer])
        h = _layer_norm(DN_ALPHA * h + ffn, ln2_g[layer], ln2_b[layer])
    return h
```

```python
import functools
import math

import numpy as np
import jax
import jax.numpy as jnp
from jax import lax
from jax.experimental import pallas as pl
from jax.experimental.pallas import tpu as pltpu

F32 = jnp.float32
BF16 = jnp.bfloat16

D_MODEL = 1024
LN_EPS = 1e-5
DEPTH = 1
DN_ALPHA = (2.0 * DEPTH) ** 0.25

DIL_GROUPS = ((128, 1), (512, 4), (2048, 16))
N_DIL_GROUPS = 3
DIL_HEADS = 4
HEAD_DIM = 128
BRANCH_WIDTH = DIL_HEADS * HEAD_DIM
DIL_BLOCK = 128
REL_BUCKETS = 32
REL_MAX_DIST = 2048

HG_HEADS = 4
HG_CHUNK = 64

MEM_HEADS = 4

N_BRANCHES = 3
IN_COLS = 10240
COL_Q, COL_K, COL_V = 0, 3, 6
COL_HG_F, COL_HG_I, COL_HG_Q, COL_HG_G = 9, 10, 11, 12
COL_MEM_Q = 13
COL_GATE = 7
N_COL_BLOCKS = IN_COLS // BRANCH_WIDTH

PEER_HEADS = 8
PEER_N_KEYS = 128
PEER_TOPK = 16
PEER_DHALF = 128

SUBLANES = 8
NEG = -1e30
BIG = 3e38
INV_SQRT2 = 0.7071067811865476

VMEM_LIMIT = 48 * 1024 * 1024


def _params(*sem):
    return pltpu.CompilerParams(dimension_semantics=sem, vmem_limit_bytes=VMEM_LIMIT)


def _dot(a, b):
    return jnp.dot(a, b, preferred_element_type=F32)


def _dot_nt(a, b):
    return lax.dot_general(a, b, (((1,), (1,)), ((), ())), preferred_element_type=F32)


def _dot_tn(a, b):
    return lax.dot_general(a, b, (((0,), (0,)), ((), ())), preferred_element_type=F32)


def _proj_kernel(x_ref, w_ref, o_ref, xb_ref):
    @pl.when(pl.program_id(1) == 0)
    def _():
        xb_ref[...] = x_ref[...].astype(BF16)

    o_ref[...] = _dot(xb_ref[...], w_ref[...]).astype(BF16)


def _proj(x2, w_bf, tm=1024, tn=1024):
    t, d = x2.shape
    n = w_bf.shape[1]
    return pl.pallas_call(
        _proj_kernel,
        grid=(t // tm, n // tn),
        in_specs=[pl.BlockSpec((tm, d), lambda i, j: (i, 0)),
                  pl.BlockSpec((d, tn), lambda i, j: (0, j))],
        out_specs=pl.BlockSpec((tm, tn), lambda i, j: (i, j)),
        out_shape=jax.ShapeDtypeStruct((t, n), BF16),
        scratch_shapes=[pltpu.VMEM((tm, d), BF16)],
        compiler_params=_params("parallel", "arbitrary"),
        name="proj",
    )(x2, w_bf)


def _t5_bucket(dist):
    max_exact = REL_BUCKETS // 2
    n = np.maximum(dist, 0)
    large = max_exact + (np.log(np.maximum(n, 1) / max_exact) / math.log(REL_MAX_DIST / max_exact)
                         * (REL_BUCKETS - max_exact)).astype(np.int32)
    large = np.minimum(large, REL_BUCKETS - 1)
    return np.where(n < max_exact, n, large).astype(np.int32)


def _bucket_tiles():
    qi = np.arange(DIL_BLOCK)[:, None]
    kj = np.arange(DIL_BLOCK)[None, :]
    tiles = []
    for window, dilation in DIL_GROUPS:
        per = []
        for delta in (qi + DIL_BLOCK - kj, qi - kj):
            ok = (delta >= 0) & (delta <= window // dilation)
            per.append(np.where(ok, _t5_bucket(delta * dilation), -1))
        tiles.append(np.stack(per))
    return np.stack(tiles).astype(np.int32)


def _bias_kernel(table_ref, bucket_ref, o_ref):
    col = pl.program_id(0)
    bucket = bucket_ref[0]
    acc = jnp.full(bucket.shape, NEG, F32)
    for b in range(REL_BUCKETS):
        acc = jnp.where(bucket == b, table_ref[b, col], acc)
    o_ref[0] = acc


def _bias_tiles(rel_bias):
    n_cols = N_DIL_GROUPS * DIL_HEADS
    buckets = jnp.asarray(_bucket_tiles())
    return pl.pallas_call(
        _bias_kernel,
        grid=(n_cols,),
        in_specs=[pl.BlockSpec(memory_space=pltpu.SMEM),
                  pl.BlockSpec((1, 2, DIL_BLOCK, DIL_BLOCK), lambda c: (c // DIL_HEADS, 0, 0, 0))],
        out_specs=pl.BlockSpec((1, 2, DIL_BLOCK, DIL_BLOCK), lambda c: (c, 0, 0, 0)),
        out_shape=jax.ShapeDtypeStruct((n_cols, 2, DIL_BLOCK, DIL_BLOCK), F32),
        compiler_params=_params("arbitrary"),
        name="rel_bias_tiles",
    )(rel_bias.astype(F32), buckets)


def _dil_kernel(q_ref, kp_ref, kc_ref, vp_ref, vc_ref, bias_ref, o_ref, lse_ref, *, nq):
    first = pl.program_id(2) == 0
    scale = HEAD_DIM ** -0.5
    lane = lax.broadcasted_iota(jnp.int32, (DIL_BLOCK, HEAD_DIM), 1)
    for j in range(nq):
        rows = slice(j * DIL_BLOCK, (j + 1) * DIL_BLOCK)
        prow = slice((j - 1) * DIL_BLOCK, j * DIL_BLOCK)
        lse_tile = jnp.zeros((DIL_BLOCK, HEAD_DIM), F32)
        for h in range(DIL_HEADS):
            cols = slice(h * HEAD_DIM, (h + 1) * HEAD_DIM)
            q = q_ref[0, rows, cols]
            if j == 0:
                k_prev, v_prev = kp_ref[0, :, cols], vp_ref[0, :, cols]
            else:
                k_prev, v_prev = kc_ref[0, prow, cols], vc_ref[0, prow, cols]
            s_prev = _dot_nt(q, k_prev) * scale + bias_ref[h, 0]
            if j == 0:
                s_prev = jnp.where(first, NEG, s_prev)
            s_cur = _dot_nt(q, kc_ref[0, rows, cols]) * scale + bias_ref[h, 1]
            m = jnp.maximum(jnp.max(s_prev, axis=-1, keepdims=True), jnp.max(s_cur, axis=-1, keepdims=True))
            p_prev = jnp.exp(s_prev - m)
            p_cur = jnp.exp(s_cur - m)
            denom = jnp.sum(p_prev, axis=-1, keepdims=True) + jnp.sum(p_cur, axis=-1, keepdims=True)
            o = _dot(p_prev.astype(BF16), v_prev) + _dot(p_cur.astype(BF16), vc_ref[0, rows, cols])
            o_ref[0, rows, cols] = (o / denom).astype(BF16)
            lse_tile = jnp.where(lane == h, m + jnp.log(denom), lse_tile)
        lse_ref[0, rows, :] = lse_tile


def _dilated_group(proj3, bias, g_idx, dilation):
    b, s, n = proj3.shape
    sub = s // dilation
    nb = sub // DIL_BLOCK
    nq = min(nb, 8)
    view = proj3.reshape(b, sub, dilation * n)
    w = BRANCH_WIDTH

    def col(section):
        return lambda bi, r, i: (bi, i, r * N_COL_BLOCKS + section + g_idx)

    def col_prev(section):
        return lambda bi, r, i: (bi, jnp.maximum(i * nq - 1, 0), r * N_COL_BLOCKS + section + g_idx)

    o, lse = pl.pallas_call(
        functools.partial(_dil_kernel, nq=nq),
        grid=(b, dilation, nb // nq),
        in_specs=[pl.BlockSpec((1, nq * DIL_BLOCK, w), col(COL_Q)),
                  pl.BlockSpec((1, DIL_BLOCK, w), col_prev(COL_K)),
                  pl.BlockSpec((1, nq * DIL_BLOCK, w), col(COL_K)),
                  pl.BlockSpec((1, DIL_BLOCK, w), col_prev(COL_V)),
                  pl.BlockSpec((1, nq * DIL_BLOCK, w), col(COL_V)),
                  pl.BlockSpec((DIL_HEADS, 2, DIL_BLOCK, DIL_BLOCK), lambda bi, r, i: (g_idx, 0, 0, 0))],
        out_specs=[pl.BlockSpec((1, nq * DIL_BLOCK, w), lambda bi, r, i: (bi, i, r)),
                   pl.BlockSpec((1, nq * DIL_BLOCK, HEAD_DIM), lambda bi, r, i: (bi, i, r))],
        out_shape=[jax.ShapeDtypeStruct((b, sub, dilation * w), BF16),
                   jax.ShapeDtypeStruct((b, sub, dilation * HEAD_DIM), F32)],
        compiler_params=_params("parallel", "parallel", "arbitrary"),
        name=f"dilated_attn_d{dilation}",
    )(view, view, view, view, view, bias)
    return o.reshape(b * s, w), lse.reshape(b * s, HEAD_DIM)


def _hgrn_kernel(f_ref, i_ref, q_ref, g_ref, lb_ref, nw_ref, o_ref, state_ref, *, layer, n_chunks):
    @pl.when(pl.program_id(1) == 0)
    def _():
        state_ref[...] = jnp.zeros_like(state_ref)

    lb_all = lb_ref[...]
    e = jnp.exp(lb_all - jnp.max(lb_all, axis=0, keepdims=True))
    lb_row = jnp.sum(e[:layer + 1], axis=0, keepdims=True) / jnp.sum(e, axis=0, keepdims=True)

    c = HG_CHUNK
    r_i = lax.broadcasted_iota(jnp.int32, (c, c), 0)
    c_i = lax.broadcasted_iota(jnp.int32, (c, c), 1)
    causal = r_i >= c_i
    tri = causal.astype(F32)
    for ch in range(n_chunks):
        rows = slice(ch * c, (ch + 1) * c)
        for h in range(HG_HEADS):
            cols = slice(h * HEAD_DIM, (h + 1) * HEAD_DIM)
            lb = lb_row[:, cols]
            f = lb + (1.0 - lb) * jax.nn.sigmoid(f_ref[0, rows, cols].astype(F32))
            key = 1.0 - f
            a = jnp.dot(tri, jnp.log(f), preferred_element_type=F32, precision=lax.Precision.HIGHEST)
            a_mid = a[c // 2 - 1:c // 2]
            a_last = a[c - 1:c]
            qh = q_ref[0, rows, cols].astype(F32)
            ih = i_ref[0, rows, cols]
            sc = _dot_nt((qh * jnp.exp(a - a_mid)).astype(BF16), (key * jnp.exp(a_mid - a)).astype(BF16))
            o = _dot(jnp.where(causal, sc, 0.0).astype(BF16), ih)
            state_t = state_ref[h]
            o = o + _dot_nt((qh * jnp.exp(a)).astype(BF16), state_t.astype(BF16))
            kd = (key * jnp.exp(a_last - a)).astype(BF16)
            state_ref[h] = jnp.exp(a_last) * state_t + _dot_tn(ih, kd)
            o = o * lax.rsqrt(jnp.mean(o * o, axis=-1, keepdims=True) + LN_EPS)
            gate = g_ref[0, rows, cols].astype(F32)
            o = o * nw_ref[:, cols] * (gate * jax.nn.sigmoid(gate))
            o_ref[0, rows, cols] = o.astype(BF16)


def _hgrn(proj3, hgrn_lb, norm_w, layer, ts=512):
    b, s, _ = proj3.shape
    w = BRANCH_WIDTH

    def col(c):
        return pl.BlockSpec((1, ts, w), lambda bi, i: (bi, i, c))

    out = pl.pallas_call(
        functools.partial(_hgrn_kernel, layer=layer, n_chunks=ts // HG_CHUNK),
        grid=(b, s // ts),
        in_specs=[col(COL_HG_F), col(COL_HG_I), col(COL_HG_Q), col(COL_HG_G),
                  pl.BlockSpec(hgrn_lb.shape, lambda bi, i: (0, 0)),
                  pl.BlockSpec((1, w), lambda bi, i: (0, 0))],
        out_specs=pl.BlockSpec((1, ts, w), lambda bi, i: (bi, i, 0)),
        out_shape=jax.ShapeDtypeStruct((b, s, w), BF16),
        scratch_shapes=[pltpu.VMEM((HG_HEADS, HEAD_DIM, HEAD_DIM), F32)],
        compiler_params=_params("parallel", "arbitrary"),
        name="hgrn2",
    )(proj3, proj3, proj3, proj3, hgrn_lb.astype(F32), norm_w.astype(F32).reshape(1, w))
    return out.reshape(b * s, w)


def _mem_kernel(q_ref, mem_ref, wkv_ref, o_ref, kv_ref):
    @pl.when(pl.program_id(1) == 0)
    def _():
        kv_ref[...] = _dot(mem_ref[0].astype(BF16), wkv_ref[...]).astype(BF16)

    scale = HEAD_DIM ** -0.5
    w = BRANCH_WIDTH
    for h in range(MEM_HEADS):
        cols = slice(h * HEAD_DIM, (h + 1) * HEAD_DIM)
        s = _dot_nt(q_ref[0, :, cols], kv_ref[:, cols]) * scale
        p = jnp.exp(s - jnp.max(s, axis=-1, keepdims=True))
        denom = jnp.sum(p, axis=-1, keepdims=True)
        o = _dot(p.astype(BF16), kv_ref[:, w + h * HEAD_DIM:w + (h + 1) * HEAD_DIM])
        o_ref[0, :, cols] = (o / denom).astype(BF16)


def _mem_attn(proj3, mem, wkv_bf, ts=1024):
    b, s, _ = proj3.shape
    m, d = mem.shape[1], mem.shape[2]
    w = BRANCH_WIDTH
    out = pl.pallas_call(
        _mem_kernel,
        grid=(b, s // ts),
        in_specs=[pl.BlockSpec((1, ts, w), lambda bi, i: (bi, i, COL_MEM_Q)),
                  pl.BlockSpec((1, m, d), lambda bi, i: (bi, 0, 0)),
                  pl.BlockSpec((d, 2 * w), lambda bi, i: (0, 0))],
        out_specs=pl.BlockSpec((1, ts, w), lambda bi, i: (bi, i, 0)),
        out_shape=jax.ShapeDtypeStruct((b, s, w), BF16),
        scratch_shapes=[pltpu.VMEM((m, 2 * w), BF16)],
        compiler_params=_params("parallel", "arbitrary"),
        name="mem_attn",
    )(proj3, mem, wkv_bf)
    return out.reshape(b * s, w)


def _layer_norm(t, g, b):
    mu = jnp.mean(t, axis=-1, keepdims=True)
    var = jnp.mean(jnp.square(t - mu), axis=-1, keepdims=True)
    return (t - mu) * lax.rsqrt(var + LN_EPS) * g + b


def _mix_kernel(x_ref, o1_ref, o2_ref, o3_ref, l1_ref, l2_ref, l3_ref, hb_ref, mc_ref,
                g1_ref, g2_ref, g3_ref, bg_ref, wb_ref, wo_ref, lg_ref, lb_ref, o_ref):
    lses = (l1_ref[...], l2_ref[...], l3_ref[...])
    mx = jnp.maximum(jnp.maximum(lses[0], lses[1]), lses[2])
    ws = [jnp.exp(l - mx) for l in lses]
    inv = 1.0 / (ws[0] + ws[1] + ws[2])
    ws = [w * inv for w in ws]
    outs = (o1_ref, o2_ref, o3_ref)
    heads = []
    for h in range(DIL_HEADS):
        cols = slice(h * HEAD_DIM, (h + 1) * HEAD_DIM)
        acc = ws[0][:, h:h + 1] * outs[0][:, cols].astype(F32)
        for g in (1, 2):
            acc = acc + ws[g][:, h:h + 1] * outs[g][:, cols].astype(F32)
        heads.append(acc)
    a_out = jnp.concatenate(heads, axis=-1).astype(BF16)

    branches = (a_out, hb_ref[...], mc_ref[...])
    gate_refs = (g1_ref, g2_ref, g3_ref)
    merged = None
    for g in range(N_BRANCHES):
        gate = jax.nn.sigmoid(gate_refs[g][...].astype(F32) + bg_ref[g:g + 1, :])
        term = gate * _dot(branches[g], wb_ref[g])
        merged = term if merged is None else merged + term
    mix = _dot(merged.astype(BF16), wo_ref[...])
    o_ref[...] = _layer_norm(DN_ALPHA * x_ref[...] + mix, lg_ref[...], lb_ref[...])


def _mix(x2, proj2, dil_outs, dil_lses, hg_out, mem_out, b_gate, wb_bf, wo_bf, ln_g, ln_b, tm=512):
    t, d = x2.shape
    w = BRANCH_WIDTH

    def rows(width):
        return pl.BlockSpec((tm, width), lambda i: (i, 0))

    def full(shape):
        return pl.BlockSpec(shape, lambda i: (0,) * len(shape))

    def gate(g):
        return pl.BlockSpec((tm, d), lambda i: (i, COL_GATE + g))

    return pl.pallas_call(
        _mix_kernel,
        grid=(t // tm,),
        in_specs=[rows(d), rows(w), rows(w), rows(w), rows(HEAD_DIM), rows(HEAD_DIM), rows(HEAD_DIM),
                  rows(w), rows(w), gate(0), gate(1), gate(2),
                  full((N_BRANCHES, d)), full((N_BRANCHES, w, d)), full((d, d)), full((1, d)), full((1, d))],
        out_specs=rows(d),
        out_shape=jax.ShapeDtypeStruct((t, d), F32),
        compiler_params=_params("parallel"),
        name="mix_ln1",
    )(x2, *dil_outs, *dil_lses, hg_out, mem_out, proj2, proj2, proj2,
      b_gate.astype(F32).reshape(N_BRANCHES, d), wb_bf, wo_bf, ln_g.astype(F32).reshape(1, d), ln_b.astype(F32).reshape(1, d))


def _top_values(s, k):
    vals = []
    for i in range(k):
        m = jnp.max(s, axis=0, keepdims=True)
        vals.append(m)
        if i + 1 < k:
            s = jnp.where(s == m, -jnp.inf, s)
    return vals


def _route_kernel(h_ref, wq_ref, keys_ref, a_ref, thr_ref, s2_ref, b_ref, s1_scr, s2_scr, *, n_lane_chunks):
    q = _dot(h_ref[...].astype(BF16), wq_ref[...]).astype(BF16)
    s1_scr[...] = _dot_nt(keys_ref[0, 0], q[:, :PEER_DHALF])
    s2_scr[...] = _dot_nt(keys_ref[0, 1], q[:, PEER_DHALF:])

    row8 = lax.broadcasted_iota(jnp.int32, (8, 128), 0)

    def chunk(c, carry):
        lanes = pl.ds(pl.multiple_of(c * 128, 128), 128)
        s1 = s1_scr[:, lanes]
        s2 = s2_scr[:, lanes]
        v1 = _top_values(s1, PEER_TOPK)
        v2 = _top_values(s2, PEER_TOPK)
        v2a = jnp.concatenate(v2[:8], axis=0)
        v2b = jnp.concatenate(v2[8:], axis=0)
        v1b = jnp.concatenate(v1[8:], axis=0)
        cands = [v1[0] + v2a, v1[0] + v2b, v1[1] + v2a]
        for k1 in range(2, 8):
            cands.append(jnp.where(row8 < PEER_TOPK // (k1 + 1), v1[k1] + v2a, -jnp.inf))
        cands.append(v1b + v2[0])
        work = list(cands)
        tau = None
        for i in range(PEER_TOPK):
            m = work[0]
            for wv in work[1:]:
                m = jnp.maximum(m, wv)
            tau = jnp.max(m, axis=0, keepdims=True)
            if i + 1 < PEER_TOPK:
                work = [jnp.where(wv == tau, -jnp.inf, wv) for wv in work]
        cmax = v1[0] + v2[0]
        z = None
        for cv in cands:
            e = jnp.where(cv >= tau, jnp.exp(cv - cmax), 0.0)
            z = e if z is None else z + e
        z = jnp.sum(z, axis=0, keepdims=True)
        thr = jnp.full(s1.shape, BIG, F32)
        for j in range(PEER_TOPK):
            thr = jnp.where(s1 + v2[j] >= tau, v2[j], thr)
        a_ref[0, :, lanes] = jnp.exp(s1 - v1[0]) / z
        thr_ref[0, :, lanes] = thr
        s2_ref[0, :, lanes] = s2
        b_ref[0, :, lanes] = jnp.exp(s2 - v2[0])
        return carry

    lax.fori_loop(0, n_lane_chunks, chunk, 0)


def _route(h1, wq_bf, keys_bf, tt=512):
    t, d = h1.shape
    nk = PEER_N_KEYS
    out = jax.ShapeDtypeStruct((PEER_HEADS, nk, t), F32)
    out_spec = pl.BlockSpec((1, nk, tt), lambda i, h: (h, 0, i))
    return pl.pallas_call(
        functools.partial(_route_kernel, n_lane_chunks=tt // 128),
        grid=(t // tt, PEER_HEADS),
        in_specs=[pl.BlockSpec((tt, d), lambda i, h: (i, 0)),
                  pl.BlockSpec((d, 2 * PEER_DHALF), lambda i, h: (0, h)),
                  pl.BlockSpec((1, 2, nk, PEER_DHALF), lambda i, h: (h, 0, 0, 0))],
        out_specs=[out_spec] * 4,
        out_shape=[out] * 4,
        scratch_shapes=[pltpu.VMEM((nk, tt), F32), pltpu.VMEM((nk, tt), F32)],
        compiler_params=_params("parallel", "arbitrary"),
        name="peer_route",
    )(h1, wq_bf, keys_bf)


def _gelu(x):
    return 0.5 * x * (1.0 + lax.erf(x * INV_SQRT2))


def _dense_kernel(h_ref, u_ref, vt_ref, a_ref, thr_ref, s2_ref, b_ref, lg_ref, lb_ref, o_ref,
                  hb_scr, act_scr, p_scr, acc_scr, *, te, tt):
    e = pl.program_id(1)

    @pl.when(e == 0)
    def _():
        hb_scr[...] = h_ref[...].astype(BF16)
        acc_scr[...] = jnp.zeros_like(acc_scr)

    act_scr[...] = _dot_nt(u_ref[...], hb_scr[...])
    key_rows = pl.ds(pl.multiple_of(e * SUBLANES, SUBLANES), SUBLANES)
    for j in range(SUBLANES):
        rows = slice(j * PEER_N_KEYS, (j + 1) * PEER_N_KEYS)
        for c in range(tt // 128):
            lanes = slice(c * 128, (c + 1) * 128)
            gate = jnp.zeros((PEER_N_KEYS, 128), F32)
            for h in range(PEER_HEADS):
                thr = thr_ref[h, key_rows, lanes][j:j + 1]
                a = a_ref[h, key_rows, lanes][j:j + 1]
                gate = gate + jnp.where(s2_ref[h, :, lanes] >= thr, b_ref[h, :, lanes], 0.0) * a
            p_scr[rows, lanes] = (gate * _gelu(act_scr[rows, lanes])).astype(BF16)
    acc_scr[...] += _dot(vt_ref[...], p_scr[...])

    @pl.when(e == pl.num_programs(1) - 1)
    def _():
        ffn = jnp.transpose(acc_scr[...])
        o_ref[...] = _layer_norm(DN_ALPHA * h_ref[...] + ffn, lg_ref[...], lb_ref[...])


def _peer_dense(h1, u_bf, vt_bf, route, ln_g, ln_b, tt=512):
    te = SUBLANES * PEER_N_KEYS
    t, d = h1.shape
    n_exp = u_bf.shape[0]
    nk = PEER_N_KEYS
    r_spec = pl.BlockSpec((PEER_HEADS, nk, tt), lambda i, e: (0, 0, i))
    return pl.pallas_call(
        functools.partial(_dense_kernel, te=te, tt=tt),
        grid=(t // tt, n_exp // te),
        in_specs=[pl.BlockSpec((tt, d), lambda i, e: (i, 0)),
                  pl.BlockSpec((te, d), lambda i, e: (e, 0)),
                  pl.BlockSpec((d, te), lambda i, e: (0, e)),
                  r_spec, r_spec, r_spec, r_spec,
                  pl.BlockSpec((1, d), lambda i, e: (0, 0)),
                  pl.BlockSpec((1, d), lambda i, e: (0, 0))],
        out_specs=pl.BlockSpec((tt, d), lambda i, e: (i, 0)),
        out_shape=jax.ShapeDtypeStruct((t, d), F32),
        scratch_shapes=[pltpu.VMEM((tt, d), BF16), pltpu.VMEM((te, tt), F32),
                        pltpu.VMEM((te, tt), BF16), pltpu.VMEM((d, tt), F32)],
        compiler_params=_params("parallel", "arbitrary"),
        name="peer_dense_ln2",
    )(h1, u_bf, vt_bf, *route, ln_g.astype(F32).reshape(1, d), ln_b.astype(F32).reshape(1, d))


def kernel(x, mem, w_in, b_gate, w_mem_kv, rel_bias, hgrn_lb, hgrn_norm_w, w_branch, w_out,
           ln1_g, ln1_b, w_peer_q, peer_sub_keys, peer_u, peer_v, ln2_g, ln2_b):
    b, s, d = x.shape
    h = x.reshape(b * s, d)
    bias = _bias_tiles(rel_bias)
    for layer in range(DEPTH):
        proj2 = _proj(h, w_in[layer].astype(BF16))
        proj3 = proj2.reshape(b, s, IN_COLS)
        dil = [_dilated_group(proj3, bias, g, dilation) for g, (_, dilation) in enumerate(DIL_GROUPS)]
        hg_out = _hgrn(proj3, hgrn_lb, hgrn_norm_w[layer], layer)
        mem_out = _mem_attn(proj3, mem, w_mem_kv[layer].astype(BF16))
        h1 = _mix(h, proj2, [o for o, _ in dil], [l for _, l in dil], hg_out, mem_out, b_gate[layer],
                  w_branch[layer].astype(BF16), w_out[layer].astype(BF16), ln1_g[layer], ln1_b[layer])
        route = _route(h1, w_peer_q[layer].astype(BF16), peer_sub_keys[layer].astype(BF16))
        h = _peer_dense(h1, peer_u[layer].astype(BF16), jnp.transpose(peer_v[layer]).astype(BF16),
                        route, ln2_g[layer], ln2_b[layer])
    return h.reshape(b, s, d)
```

```python
import functools
import math

import numpy as np
import jax
import jax.numpy as jnp
from jax import lax
from jax.experimental import pallas as pl
from jax.experimental.pallas import tpu as pltpu

F32 = jnp.float32
BF16 = jnp.bfloat16

D_MODEL = 1024
LN_EPS = 1e-5
DEPTH = 1
DN_ALPHA = (2.0 * DEPTH) ** 0.25

DIL_GROUPS = ((128, 1), (512, 4), (2048, 16))
N_DIL_GROUPS = 3
DIL_HEADS = 4
HEAD_DIM = 128
BRANCH_WIDTH = DIL_HEADS * HEAD_DIM
DIL_BLOCK = 128
REL_BUCKETS = 32
REL_MAX_DIST = 2048

HG_HEADS = 4
HG_CHUNK = 64

MEM_HEADS = 4

N_BRANCHES = 3
IN_COLS = 10240
COL_Q, COL_K, COL_V = 0, 3, 6
COL_HG_F, COL_HG_I, COL_HG_Q, COL_HG_G = 9, 10, 11, 12
COL_MEM_Q = 13
COL_GATE = 7
N_COL_BLOCKS = IN_COLS // BRANCH_WIDTH

PEER_HEADS = 8
PEER_N_KEYS = 128
PEER_TOPK = 16
PEER_DHALF = 128

LANES = 128
SUBLANES = 8
NEG = -1e30
NO_RANK = 64.0
INV_SQRT2 = 0.7071067811865476

VMEM_LIMIT = 48 * 1024 * 1024


def _params(*sem):
    return pltpu.CompilerParams(dimension_semantics=sem, vmem_limit_bytes=VMEM_LIMIT)


def _dot(a, b):
    return jnp.dot(a, b, preferred_element_type=F32)


def _dot_nt(a, b):
    return lax.dot_general(a, b, (((1,), (1,)), ((), ())), preferred_element_type=F32)


def _dot_tn(a, b):
    return lax.dot_general(a, b, (((0,), (0,)), ((), ())), preferred_element_type=F32)


def _strided_section(j, g_idx):
    return (j >= COL_K + g_idx).astype(jnp.int32) + (j >= COL_V + g_idx).astype(jnp.int32)


def _proj_kernel(x_ref, w_ref, o_ref, y4_ref, y16_ref, xb_ref, acc_ref):
    j = pl.program_id(1)

    @pl.when(j == 0)
    def _():
        xb_ref[...] = x_ref[...].astype(BF16)

    res = _dot(xb_ref[...], w_ref[...])
    o_ref[...] = res.astype(BF16)
    n_lane_blocks = acc_ref.shape[0]
    for c in range(n_lane_blocks):
        acc_ref[c] = res[:, c * LANES:(c + 1) * LANES]

    w = BRANCH_WIDTH
    for g_idx, y_ref in ((1, y4_ref), (2, y16_ref)):
        dilation = DIL_GROUPS[g_idx][1]
        sub_rows = acc_ref.shape[1] // dilation

        @pl.when((j == COL_Q + g_idx) | (j == COL_K + g_idx) | (j == COL_V + g_idx))
        def _(y_ref=y_ref, dilation=dilation, sub_rows=sub_rows):
            for r in range(dilation):
                for c in range(n_lane_blocks):
                    cols = slice(r * w + c * LANES, r * w + (c + 1) * LANES)
                    y_ref[:, cols] = acc_ref[c, pl.ds(r, sub_rows, stride=dilation), :].astype(BF16)


def _proj(x2, w_bf, tm=2048):
    t, d = x2.shape
    n = w_bf.shape[1]
    tn = BRANCH_WIDTH

    def strided(g_idx):
        dilation = DIL_GROUPS[g_idx][1]
        spec = pl.BlockSpec((tm // dilation, dilation * tn), lambda i, j: (i, _strided_section(j, g_idx)))
        return spec, jax.ShapeDtypeStruct((t // dilation, 3 * dilation * tn), BF16)

    (spec4, shape4), (spec16, shape16) = strided(1), strided(2)
    return pl.pallas_call(
        _proj_kernel,
        grid=(t // tm, n // tn),
        in_specs=[pl.BlockSpec((tm, d), lambda i, j: (i, 0)),
                  pl.BlockSpec((d, tn), lambda i, j: (0, j))],
        out_specs=[pl.BlockSpec((tm, tn), lambda i, j: (i, j)), spec4, spec16],
        out_shape=[jax.ShapeDtypeStruct((t, n), BF16), shape4, shape16],
        scratch_shapes=[pltpu.VMEM((tm, d), BF16), pltpu.VMEM((tn // LANES, tm, LANES), F32)],
        compiler_params=_params("parallel", "arbitrary"),
        name="proj",
    )(x2, w_bf)


def _t5_bucket(dist):
    max_exact = REL_BUCKETS // 2
    n = np.maximum(dist, 0)
    large = max_exact + (np.log(np.maximum(n, 1) / max_exact) / math.log(REL_MAX_DIST / max_exact)
                         * (REL_BUCKETS - max_exact)).astype(np.int32)
    large = np.minimum(large, REL_BUCKETS - 1)
    return np.where(n < max_exact, n, large).astype(np.int32)


def _bucket_tiles():
    qi = np.arange(DIL_BLOCK)[:, None]
    kj = np.arange(DIL_BLOCK)[None, :]
    tiles = []
    for window, dilation in DIL_GROUPS:
        per = []
        for delta in (qi + DIL_BLOCK - kj, qi - kj):
            ok = (delta >= 0) & (delta <= window // dilation)
            per.append(np.where(ok, _t5_bucket(delta * dilation), -1))
        tiles.append(np.stack(per))
    return np.stack(tiles).astype(np.int32)


def _bias_kernel(table_ref, bucket_ref, o_ref):
    col = pl.program_id(0)
    bucket = bucket_ref[0]
    acc = jnp.full(bucket.shape, NEG, F32)
    for b in range(REL_BUCKETS):
        acc = jnp.where(bucket == b, table_ref[b, col], acc)
    o_ref[0] = acc


def _bias_tiles(rel_bias):
    n_cols = N_DIL_GROUPS * DIL_HEADS
    buckets = jnp.asarray(_bucket_tiles())
    return pl.pallas_call(
        _bias_kernel,
        grid=(n_cols,),
        in_specs=[pl.BlockSpec(memory_space=pltpu.SMEM),
                  pl.BlockSpec((1, 2, DIL_BLOCK, DIL_BLOCK), lambda c: (c // DIL_HEADS, 0, 0, 0))],
        out_specs=pl.BlockSpec((1, 2, DIL_BLOCK, DIL_BLOCK), lambda c: (c, 0, 0, 0)),
        out_shape=jax.ShapeDtypeStruct((n_cols, 2, DIL_BLOCK, DIL_BLOCK), F32),
        compiler_params=_params("arbitrary"),
        name="rel_bias_tiles",
    )(rel_bias.astype(F32), buckets)


def _dil_kernel(q_ref, kp_ref, kc_ref, vp_ref, vc_ref, bias_ref, o_ref, lse_ref, *, nq):
    first = pl.program_id(2) == 0
    scale = HEAD_DIM ** -0.5
    lane = lax.broadcasted_iota(jnp.int32, (DIL_BLOCK, HEAD_DIM), 1)
    for j in range(nq):
        rows = slice(j * DIL_BLOCK, (j + 1) * DIL_BLOCK)
        prow = slice((j - 1) * DIL_BLOCK, j * DIL_BLOCK)
        lse_tile = jnp.zeros((DIL_BLOCK, HEAD_DIM), F32)
        for h in range(DIL_HEADS):
            cols = slice(h * HEAD_DIM, (h + 1) * HEAD_DIM)
            q = q_ref[0, rows, cols]
            if j == 0:
                k_prev, v_prev = kp_ref[0, :, cols], vp_ref[0, :, cols]
            else:
                k_prev, v_prev = kc_ref[0, prow, cols], vc_ref[0, prow, cols]
            s_prev = _dot_nt(q, k_prev) * scale + bias_ref[h, 0]
            if j == 0:
                s_prev = jnp.where(first, NEG, s_prev)
            s_cur = _dot_nt(q, kc_ref[0, rows, cols]) * scale + bias_ref[h, 1]
            m = jnp.maximum(jnp.max(s_prev, axis=-1, keepdims=True), jnp.max(s_cur, axis=-1, keepdims=True))
            p_prev = jnp.exp(s_prev - m)
            p_cur = jnp.exp(s_cur - m)
            denom = jnp.sum(p_prev, axis=-1, keepdims=True) + jnp.sum(p_cur, axis=-1, keepdims=True)
            o = _dot(p_prev.astype(BF16), v_prev) + _dot(p_cur.astype(BF16), vc_ref[0, rows, cols])
            o_ref[0, rows, cols] = (o / denom).astype(BF16)
            lse_tile = jnp.where(lane == h, m + jnp.log(denom), lse_tile)
        lse_ref[0, rows, :] = lse_tile


def _dilated_group(view, col_block, bias, g_idx, dilation):
    b, sub, _ = view.shape
    s = sub * dilation
    nb = sub // DIL_BLOCK
    nq = min(nb, 8)
    w = BRANCH_WIDTH

    def col(section):
        return lambda bi, r, i: (bi, i, col_block(section, r))

    def col_prev(section):
        return lambda bi, r, i: (bi, jnp.maximum(i * nq - 1, 0), col_block(section, r))

    o, lse = pl.pallas_call(
        functools.partial(_dil_kernel, nq=nq),
        grid=(b, dilation, nb // nq),
        in_specs=[pl.BlockSpec((1, nq * DIL_BLOCK, w), col(0)),
                  pl.BlockSpec((1, DIL_BLOCK, w), col_prev(1)),
                  pl.BlockSpec((1, nq * DIL_BLOCK, w), col(1)),
                  pl.BlockSpec((1, DIL_BLOCK, w), col_prev(2)),
                  pl.BlockSpec((1, nq * DIL_BLOCK, w), col(2)),
                  pl.BlockSpec((DIL_HEADS, 2, DIL_BLOCK, DIL_BLOCK), lambda bi, r, i: (g_idx, 0, 0, 0))],
        out_specs=[pl.BlockSpec((1, nq * DIL_BLOCK, w), lambda bi, r, i: (bi, i, r)),
                   pl.BlockSpec((1, nq * DIL_BLOCK, HEAD_DIM), lambda bi, r, i: (bi, i, r))],
        out_shape=[jax.ShapeDtypeStruct((b, sub, dilation * w), BF16),
                   jax.ShapeDtypeStruct((b, sub, dilation * HEAD_DIM), F32)],
        compiler_params=_params("parallel", "parallel", "arbitrary"),
        name=f"dilated_attn_d{dilation}",
    )(view, view, view, view, view, bias)
    return o.reshape(b * s, w), lse.reshape(b * s, HEAD_DIM)


def _hgrn_kernel(f_ref, i_ref, q_ref, g_ref, lb_ref, nw_ref, o_ref, state_ref, *, layer, n_chunks):
    @pl.when(pl.program_id(1) == 0)
    def _():
        state_ref[...] = jnp.zeros_like(state_ref)

    lb_all = lb_ref[...]
    e = jnp.exp(lb_all - jnp.max(lb_all, axis=0, keepdims=True))
    lb_row = jnp.sum(e[:layer + 1], axis=0, keepdims=True) / jnp.sum(e, axis=0, keepdims=True)

    c = HG_CHUNK
    r_i = lax.broadcasted_iota(jnp.int32, (c, c), 0)
    c_i = lax.broadcasted_iota(jnp.int32, (c, c), 1)
    causal = r_i >= c_i
    tri = causal.astype(F32)
    for ch in range(n_chunks):
        rows = slice(ch * c, (ch + 1) * c)
        for h in range(HG_HEADS):
            cols = slice(h * HEAD_DIM, (h + 1) * HEAD_DIM)
            lb = lb_row[:, cols]
            f = lb + (1.0 - lb) * jax.nn.sigmoid(f_ref[0, rows, cols].astype(F32))
            key = 1.0 - f
            a = jnp.dot(tri, jnp.log(f), preferred_element_type=F32, precision=lax.Precision.HIGHEST)
            a_mid = a[c // 2 - 1:c // 2]
            a_last = a[c - 1:c]
            qh = q_ref[0, rows, cols].astype(F32)
            ih = i_ref[0, rows, cols]
            sc = _dot_nt((qh * jnp.exp(a - a_mid)).astype(BF16), (key * jnp.exp(a_mid - a)).astype(BF16))
            o = _dot(jnp.where(causal, sc, 0.0).astype(BF16), ih)
            state_t = state_ref[h]
            o = o + _dot_nt((qh * jnp.exp(a)).astype(BF16), state_t.astype(BF16))
            kd = (key * jnp.exp(a_last - a)).astype(BF16)
            state_ref[h] = jnp.exp(a_last) * state_t + _dot_tn(ih, kd)
            o = o * lax.rsqrt(jnp.mean(o * o, axis=-1, keepdims=True) + LN_EPS)
            gate = g_ref[0, rows, cols].astype(F32)
            o = o * nw_ref[:, cols] * (gate * jax.nn.sigmoid(gate))
            o_ref[0, rows, cols] = o.astype(BF16)


def _hgrn(proj3, hgrn_lb, norm_w, layer, ts=512):
    b, s, _ = proj3.shape
    w = BRANCH_WIDTH

    def col(c):
        return pl.BlockSpec((1, ts, w), lambda bi, i: (bi, i, c))

    out = pl.pallas_call(
        functools.partial(_hgrn_kernel, layer=layer, n_chunks=ts // HG_CHUNK),
        grid=(b, s // ts),
        in_specs=[col(COL_HG_F), col(COL_HG_I), col(COL_HG_Q), col(COL_HG_G),
                  pl.BlockSpec(hgrn_lb.shape, lambda bi, i: (0, 0)),
                  pl.BlockSpec((1, w), lambda bi, i: (0, 0))],
        out_specs=pl.BlockSpec((1, ts, w), lambda bi, i: (bi, i, 0)),
        out_shape=jax.ShapeDtypeStruct((b, s, w), BF16),
        scratch_shapes=[pltpu.VMEM((HG_HEADS, HEAD_DIM, HEAD_DIM), F32)],
        compiler_params=_params("parallel", "arbitrary"),
        name="hgrn2",
    )(proj3, proj3, proj3, proj3, hgrn_lb.astype(F32), norm_w.astype(F32).reshape(1, w))
    return out.reshape(b * s, w)


def _mem_kernel(q_ref, mem_ref, wkv_ref, o_ref, kv_ref):
    @pl.when(pl.program_id(1) == 0)
    def _():
        kv_ref[...] = _dot(mem_ref[0].astype(BF16), wkv_ref[...]).astype(BF16)

    scale = HEAD_DIM ** -0.5
    w = BRANCH_WIDTH
    for h in range(MEM_HEADS):
        cols = slice(h * HEAD_DIM, (h + 1) * HEAD_DIM)
        s = _dot_nt(q_ref[0, :, cols], kv_ref[:, cols]) * scale
        p = jnp.exp(s - jnp.max(s, axis=-1, keepdims=True))
        denom = jnp.sum(p, axis=-1, keepdims=True)
        o = _dot(p.astype(BF16), kv_ref[:, w + h * HEAD_DIM:w + (h + 1) * HEAD_DIM])
        o_ref[0, :, cols] = (o / denom).astype(BF16)


def _mem_attn(proj3, mem, wkv_bf, ts=1024):
    b, s, _ = proj3.shape
    m, d = mem.shape[1], mem.shape[2]
    w = BRANCH_WIDTH
    out = pl.pallas_call(
        _mem_kernel,
        grid=(b, s // ts),
        in_specs=[pl.BlockSpec((1, ts, w), lambda bi, i: (bi, i, COL_MEM_Q)),
                  pl.BlockSpec((1, m, d), lambda bi, i: (bi, 0, 0)),
                  pl.BlockSpec((d, 2 * w), lambda bi, i: (0, 0))],
        out_specs=pl.BlockSpec((1, ts, w), lambda bi, i: (bi, i, 0)),
        out_shape=jax.ShapeDtypeStruct((b, s, w), BF16),
        scratch_shapes=[pltpu.VMEM((m, 2 * w), BF16)],
        compiler_params=_params("parallel", "arbitrary"),
        name="mem_attn",
    )(proj3, mem, wkv_bf)
    return out.reshape(b * s, w)


def _layer_norm(t, g, b):
    mu = jnp.mean(t, axis=-1, keepdims=True)
    var = jnp.mean(jnp.square(t - mu), axis=-1, keepdims=True)
    return (t - mu) * lax.rsqrt(var + LN_EPS) * g + b


def _mix_kernel(x_ref, o1_ref, o2_ref, o3_ref, l1_ref, l2_ref, l3_ref, hb_ref, mc_ref,
                g1_ref, g2_ref, g3_ref, bg_ref, wb_ref, wo_ref, lg_ref, lb_ref, o_ref):
    lses = (l1_ref[...], l2_ref[...], l3_ref[...])
    mx = jnp.maximum(jnp.maximum(lses[0], lses[1]), lses[2])
    ws = [jnp.exp(l - mx) for l in lses]
    inv = 1.0 / (ws[0] + ws[1] + ws[2])
    ws = [w * inv for w in ws]
    outs = (o1_ref, o2_ref, o3_ref)
    heads = []
    for h in range(DIL_HEADS):
        cols = slice(h * HEAD_DIM, (h + 1) * HEAD_DIM)
        acc = ws[0][:, h:h + 1] * outs[0][:, cols].astype(F32)
        for g in (1, 2):
            acc = acc + ws[g][:, h:h + 1] * outs[g][:, cols].astype(F32)
        heads.append(acc)
    a_out = jnp.concatenate(heads, axis=-1).astype(BF16)

    branches = (a_out, hb_ref[...], mc_ref[...])
    gate_refs = (g1_ref, g2_ref, g3_ref)
    merged = None
    for g in range(N_BRANCHES):
        gate = jax.nn.sigmoid(gate_refs[g][...].astype(F32) + bg_ref[g:g + 1, :])
        term = gate * _dot(branches[g], wb_ref[g])
        merged = term if merged is None else merged + term
    mix = _dot(merged.astype(BF16), wo_ref[...])
    o_ref[...] = _layer_norm(DN_ALPHA * x_ref[...] + mix, lg_ref[...], lb_ref[...])


def _mix(x2, proj2, dil_outs, dil_lses, hg_out, mem_out, b_gate, wb_bf, wo_bf, ln_g, ln_b, tm=512):
    t, d = x2.shape
    w = BRANCH_WIDTH

    def rows(width):
        return pl.BlockSpec((tm, width), lambda i: (i, 0))

    def full(shape):
        return pl.BlockSpec(shape, lambda i: (0,) * len(shape))

    def gate(g):
        return pl.BlockSpec((tm, d), lambda i: (i, COL_GATE + g))

    return pl.pallas_call(
        _mix_kernel,
        grid=(t // tm,),
        in_specs=[rows(d), rows(w), rows(w), rows(w), rows(HEAD_DIM), rows(HEAD_DIM), rows(HEAD_DIM),
                  rows(w), rows(w), gate(0), gate(1), gate(2),
                  full((N_BRANCHES, d)), full((N_BRANCHES, w, d)), full((d, d)), full((1, d)), full((1, d))],
        out_specs=rows(d),
        out_shape=jax.ShapeDtypeStruct((t, d), F32),
        compiler_params=_params("parallel"),
        name="mix_ln1",
    )(x2, *dil_outs, *dil_lses, hg_out, mem_out, proj2, proj2, proj2,
      b_gate.astype(F32).reshape(N_BRANCHES, d), wb_bf, wo_bf, ln_g.astype(F32).reshape(1, d), ln_b.astype(F32).reshape(1, d))


def _top_values(s, k):
    vals = []
    rank = jnp.full(s.shape, NO_RANK, F32)
    for i in range(k):
        m = jnp.max(s, axis=0, keepdims=True)
        vals.append(m)
        hit = s == m
        rank = jnp.where(hit, float(i), rank)
        if i + 1 < k:
            s = jnp.where(hit, -jnp.inf, s)
    return vals, rank


def _paired_bf16_words(x):
    bits = pltpu.bitcast(x.astype(BF16).astype(F32), jnp.uint32)
    return bits | (bits >> 16)


def _route_kernel(h_ref, wq_ref, keys_ref, aw_ref, jw_ref, r2_ref, b_ref, s1_scr, s2_scr, *, n_lane_chunks):
    q = _dot(h_ref[...].astype(BF16), wq_ref[...]).astype(BF16)
    s1_scr[...] = _dot_nt(keys_ref[0, 0], q[:, :PEER_DHALF])
    s2_scr[...] = _dot_nt(keys_ref[0, 1], q[:, PEER_DHALF:])

    row8 = lax.broadcasted_iota(jnp.int32, (8, 128), 0)

    def chunk(c, carry):
        lanes = pl.ds(pl.multiple_of(c * 128, 128), 128)
        s1 = s1_scr[:, lanes]
        s2 = s2_scr[:, lanes]
        v1, _ = _top_values(s1, PEER_TOPK)
        v2, rank2 = _top_values(s2, PEER_TOPK)
        v2a = jnp.concatenate(v2[:8], axis=0)
        v2b = jnp.concatenate(v2[8:], axis=0)
        v1b = jnp.concatenate(v1[8:], axis=0)
        cands = [v1[0] + v2a, v1[0] + v2b, v1[1] + v2a]
        for k1 in range(2, 8):
            cands.append(jnp.where(row8 < PEER_TOPK // (k1 + 1), v1[k1] + v2a, -jnp.inf))
        cands.append(v1b + v2[0])
        work = list(cands)
        tau = None
        for i in range(PEER_TOPK):
            m = work[0]
            for wv in work[1:]:
                m = jnp.maximum(m, wv)
            tau = jnp.max(m, axis=0, keepdims=True)
            if i + 1 < PEER_TOPK:
                work = [jnp.where(wv == tau, -jnp.inf, wv) for wv in work]
        cmax = v1[0] + v2[0]
        z = None
        for cv in cands:
            e = jnp.where(cv >= tau, jnp.exp(cv - cmax), 0.0)
            z = e if z is None else z + e
        z = jnp.sum(z, axis=0, keepdims=True)
        last = jnp.full(s1.shape, -1.0, F32)
        for j in range(PEER_TOPK):
            last = jnp.where(s1 + v2[j] >= tau, float(j), last)
        aw_ref[0, :, lanes] = _paired_bf16_words(jnp.exp(s1 - v1[0]) / z)
        jw_ref[0, :, lanes] = _paired_bf16_words(last)
        r2_ref[0, :, lanes] = pltpu.bitcast(rank2.astype(BF16), jnp.uint32)
        b_ref[0, :, lanes] = pltpu.bitcast(jnp.exp(s2 - v2[0]).astype(BF16), jnp.uint32)
        return carry

    lax.fori_loop(0, n_lane_chunks, chunk, 0)


def _route(h1, wq_bf, keys_bf, tt=512):
    t, d = h1.shape
    nk = PEER_N_KEYS
    words = jax.ShapeDtypeStruct((PEER_HEADS, nk, t), jnp.uint32)
    pairs = jax.ShapeDtypeStruct((PEER_HEADS, nk // 2, t), jnp.uint32)
    out_spec = pl.BlockSpec((1, nk, tt), lambda i, h: (h, 0, i))
    pair_spec = pl.BlockSpec((1, nk // 2, tt), lambda i, h: (h, 0, i))
    return pl.pallas_call(
        functools.partial(_route_kernel, n_lane_chunks=tt // 128),
        grid=(t // tt, PEER_HEADS),
        in_specs=[pl.BlockSpec((tt, d), lambda i, h: (i, 0)),
                  pl.BlockSpec((d, 2 * PEER_DHALF), lambda i, h: (0, h)),
                  pl.BlockSpec((1, 2, nk, PEER_DHALF), lambda i, h: (h, 0, 0, 0))],
        out_specs=[out_spec, out_spec, pair_spec, pair_spec],
        out_shape=[words, words, pairs, pairs],
        scratch_shapes=[pltpu.VMEM((nk, tt), F32), pltpu.VMEM((nk, tt), F32)],
        compiler_params=_params("parallel", "arbitrary"),
        name="peer_route",
    )(h1, wq_bf, keys_bf)


def _gelu(x):
    return 0.5 * x * (1.0 + lax.erf(x * INV_SQRT2))


BF16_ROWS = 2 * SUBLANES


def _dense_kernel(h_ref, u_ref, vt_ref, aw_ref, jw_ref, r2_ref, b_ref, lg_ref, lb_ref, o_ref,
                  hb_scr, act0_scr, act1_scr, p0_scr, p1_scr, acc_scr, *, n_blocks, tt):
    s = pl.program_id(1)

    @pl.when(s == 0)
    def _():
        hb_scr[...] = h_ref[...].astype(BF16)
        acc_scr[...] = jnp.zeros_like(acc_scr)
        act1_scr[...] = jnp.zeros_like(act1_scr)
        p0_scr[...] = jnp.zeros_like(p0_scr)

    gate_valid = (s >= 1) & (s <= n_blocks)
    block = jnp.clip(s - 1, 0, n_blocks - 1)
    key_rows = pl.ds(pl.multiple_of(block * SUBLANES, SUBLANES), SUBLANES)
    zero = jnp.zeros((BF16_ROWS, LANES), BF16)

    def step(act_w, act_r, p_w, p_r):
        n_slices = 4
        for j in range(SUBLANES):
            if j % (SUBLANES // n_slices) == 0:
                q = j // (SUBLANES // n_slices)
                m1 = slice(q * (act_w.shape[0] // n_slices), (q + 1) * (act_w.shape[0] // n_slices))
                act_w[m1, :] = _dot_nt(u_ref[m1, :], hb_scr[...])
                m2 = slice(q * (acc_scr.shape[0] // n_slices), (q + 1) * (acc_scr.shape[0] // n_slices))
                acc_scr[m2, :] += _dot(vt_ref[m2, :], p_r[...])
            for c in range(tt // LANES):
                lanes = slice(c * LANES, (c + 1) * LANES)
                gate = [None] * (PEER_N_KEYS // BF16_ROWS)
                for h in range(PEER_HEADS):
                    last = jnp.broadcast_to(jw_ref[h, key_rows, lanes][j:j + 1], (SUBLANES, LANES))
                    a = jnp.broadcast_to(aw_ref[h, key_rows, lanes][j:j + 1], (SUBLANES, LANES))
                    last = pltpu.bitcast(last, BF16)
                    a = pltpu.bitcast(a, BF16)
                    for v in range(len(gate)):
                        words = slice(v * SUBLANES, (v + 1) * SUBLANES)
                        rank = pltpu.bitcast(r2_ref[h, words, lanes], BF16)
                        b = pltpu.bitcast(b_ref[h, words, lanes], BF16)
                        term = jnp.where(rank <= last, b, zero) * a
                        gate[v] = term if gate[v] is None else gate[v] + term
                rows = slice(j * PEER_N_KEYS, (j + 1) * PEER_N_KEYS)
                p = jnp.concatenate(gate, axis=0) * _gelu(act_r[rows, lanes]).astype(BF16)
                p_w[rows, lanes] = jnp.where(gate_valid, p, jnp.zeros_like(p))

    @pl.when(s % 2 == 0)
    def _():
        step(act0_scr, act1_scr, p1_scr, p0_scr)

    @pl.when(s % 2 == 1)
    def _():
        step(act1_scr, act0_scr, p0_scr, p1_scr)

    @pl.when(s == pl.num_programs(1) - 1)
    def _():
        ffn = jnp.transpose(acc_scr[...])
        o_ref[...] = _layer_norm(DN_ALPHA * h_ref[...] + ffn, lg_ref[...], lb_ref[...])


def _peer_dense(h1, u_bf, vt_bf, route, ln_g, ln_b, tt=512):
    te = SUBLANES * PEER_N_KEYS
    t, d = h1.shape
    n_blocks = u_bf.shape[0] // te
    nk = PEER_N_KEYS
    r_spec = pl.BlockSpec((PEER_HEADS, nk, tt), lambda i, s: (0, 0, i))
    pair_spec = pl.BlockSpec((PEER_HEADS, nk // 2, tt), lambda i, s: (0, 0, i))
    return pl.pallas_call(
        functools.partial(_dense_kernel, n_blocks=n_blocks, tt=tt),
        grid=(t // tt, n_blocks + 2),
        in_specs=[pl.BlockSpec((tt, d), lambda i, s: (i, 0)),
                  pl.BlockSpec((te, d), lambda i, s: (jnp.minimum(s, n_blocks - 1), 0)),
                  pl.BlockSpec((d, te), lambda i, s: (0, jnp.clip(s - 2, 0, n_blocks - 1))),
                  r_spec, r_spec, pair_spec, pair_spec,
                  pl.BlockSpec((1, d), lambda i, s: (0, 0)),
                  pl.BlockSpec((1, d), lambda i, s: (0, 0))],
        out_specs=pl.BlockSpec((tt, d), lambda i, s: (i, 0)),
        out_shape=jax.ShapeDtypeStruct((t, d), F32),
        scratch_shapes=[pltpu.VMEM((tt, d), BF16), pltpu.VMEM((te, tt), F32), pltpu.VMEM((te, tt), F32),
                        pltpu.VMEM((te, tt), BF16), pltpu.VMEM((te, tt), BF16), pltpu.VMEM((d, tt), F32)],
        compiler_params=_params("parallel", "arbitrary"),
        name="peer_dense_ln2",
    )(h1, u_bf, vt_bf, *route, ln_g.astype(F32).reshape(1, d), ln_b.astype(F32).reshape(1, d))


def kernel(x, mem, w_in, b_gate, w_mem_kv, rel_bias, hgrn_lb, hgrn_norm_w, w_branch, w_out,
           ln1_g, ln1_b, w_peer_q, peer_sub_keys, peer_u, peer_v, ln2_g, ln2_b):
    b, s, d = x.shape
    h = x.reshape(b * s, d)
    bias = _bias_tiles(rel_bias)
    for layer in range(DEPTH):
        proj2, y4, y16 = _proj(h, w_in[layer].astype(BF16))
        proj3 = proj2.reshape(b, s, IN_COLS)
        dil = [_dilated_group(proj3, lambda sec, r: 3 * sec, bias, 0, 1)]
        for g, y in ((1, y4), (2, y16)):
            dilation = DIL_GROUPS[g][1]
            view = y.reshape(b, s // dilation, 3 * dilation * BRANCH_WIDTH)
            dil.append(_dilated_group(view, lambda sec, r, dilation=dilation: sec * dilation + r, bias, g, dilation))
        hg_out = _hgrn(proj3, hgrn_lb, hgrn_norm_w[layer], layer)
        mem_out = _mem_attn(proj3, mem, w_mem_kv[layer].astype(BF16))
        h1 = _mix(h, proj2, [o for o, _ in dil], [l for _, l in dil], hg_out, mem_out, b_gate[layer],
                  w_branch[layer].astype(BF16), w_out[layer].astype(BF16), ln1_g[layer], ln1_b[layer])
        route = _route(h1, w_peer_q[layer].astype(BF16), peer_sub_keys[layer].astype(BF16))
        h = _peer_dense(h1, peer_u[layer].astype(BF16), jnp.transpose(peer_v[layer]).astype(BF16),
                        route, ln2_g[layer], ln2_b[layer])
    return h.reshape(b, s, d)
```
